```python
import jax, jax.numpy as jnp
from jax import lax
import numpy as np

D_MODEL = 2048
BATCH = 16
SEQ = 256
DEPTH = 2
DEC_BATCH = 8
DEC_SEQ = 2048
PAST_LEN = 256

GRID_W = 64
DN_HEADS = 8
DN_DK = 128
DN_DV = 128
DN_QK = DN_HEADS * DN_DK
DN_VW = DN_HEADS * DN_DV
DN_CONV = 5
DN_CHUNK = 64
SGU_WIDTH = 1024
SGU_GROUPS = 4
SGU_CHUNK = 128
POOL_WIDTH = 1024
POOL_GROUPS = 4
POOL_WINDOWS = (2, 4, 8, 16)
D_FF = 5632
FFN_CONV = 3
N_BRANCH = 3
EPS = 1e-6
IN_SIZES = (2 * DN_QK + DN_VW, DN_VW, 2 * DN_HEADS, 2 * DN_HEADS, SGU_WIDTH, SGU_WIDTH, POOL_WIDTH, N_BRANCH * D_MODEL)
D_IN = 2 * DN_QK + 2 * DN_VW + 4 * DN_HEADS + 2 * SGU_WIDTH + POOL_WIDTH + N_BRANCH * D_MODEL

kernel_name = "hybrid_flow_backbone_step"


def rmsnorm(x, w):
    xf = x.astype(jnp.float32)
    y = xf * lax.rsqrt(jnp.mean(xf * xf, axis=-1, keepdims=True) + EPS)
    return (y * w.astype(jnp.float32)).astype(x.dtype)


def l2norm(x):
    return x * lax.rsqrt(jnp.sum(x * x, axis=-1, keepdims=True) + EPS)


def conv1d_centred(x, k):
    return lax.conv_general_dilated(x, k[:, None, :], window_strides=(1,), padding='SAME',
                                    dimension_numbers=('NWC', 'WIO', 'NWC'),
                                    feature_group_count=x.shape[-1])


def depthwise_conv2d(x, k, rows, cols):
    B, L, C = x.shape
    y = lax.conv_general_dilated(x.reshape(B, rows, cols, C), k[:, :, None, :], window_strides=(1, 1),
                                 padding='SAME', dimension_numbers=('NHWC', 'HWIO', 'NHWC'),
                                 feature_group_count=C)
    return y.reshape(B, L, C)


def box_mean_minus_self(x, rows, cols, w):
    B, L, C = x.shape
    xg = x.reshape(B, rows, cols, C).astype(jnp.float32)
    sat = jnp.pad(jnp.cumsum(jnp.cumsum(xg, axis=1), axis=2), ((0, 0), (1, 0), (1, 0), (0, 0)))
    half = w // 2
    ri = jnp.arange(rows)
    ci = jnp.arange(cols)
    r0, r1 = jnp.clip(ri - half, 0, rows), jnp.clip(ri + half, 0, rows)
    c0, c1 = jnp.clip(ci - half, 0, cols), jnp.clip(ci + half, 0, cols)
    corner = lambda r, c: sat[:, r][:, :, c]
    s = corner(r1, c1) - corner(r0, c1) - corner(r1, c0) + corner(r0, c0)
    cnt = ((r1 - r0)[:, None] * (c1 - c0)[None, :]).astype(jnp.float32)
    return (s / cnt[None, :, :, None] - xg).reshape(B, L, C).astype(x.dtype)


def _chunk(t, c):
    B, L, H = t.shape[:3]
    t = t.reshape((B, L // c, c, H) + t.shape[3:])
    return jnp.moveaxis(t, 3, 2)


def gated_delta_rule(q, k, v, g, beta, s0):
    B, L, H, _ = q.shape
    C = DN_CHUNK
    q, k, v = _chunk(q, C), _chunk(k, C), _chunk(v, C)
    g, beta = _chunk(g, C), _chunk(beta, C)
    gc = jnp.cumsum(g, axis=-1)
    lower = jnp.tril(jnp.ones((C, C), bool))
    strict = jnp.tril(jnp.ones((C, C), bool), -1)
    decay = jnp.exp(jnp.where(lower, gc[..., :, None] - gc[..., None, :], -jnp.inf))
    kb = k * beta[..., None]
    a_mat = jnp.where(strict, jnp.einsum('bnhik,bnhjk->bnhij', kb, k) * decay, 0.0)
    rhs = jnp.concatenate([v * beta[..., None], kb * jnp.exp(gc)[..., None]], axis=-1)
    sol = lax.linalg.triangular_solve(jnp.eye(C, dtype=q.dtype) + a_mat, rhs, left_side=True, lower=True)
    u, w = sol[..., :DN_DV], sol[..., DN_DV:]
    qk = jnp.einsum('bnhik,bnhjk->bnhij', q, k) * decay

    def step(s, blk):
        qn, kn, un, wn, gn, qkn = blk
        v_new = un - jnp.einsum('bhck,bhkv->bhcv', wn, s)
        o = (jnp.einsum('bhck,bhkv->bhcv', qn * jnp.exp(gn)[..., None], s)
             + jnp.einsum('bhij,bhjv->bhiv', qkn, v_new))
        g_last = gn[..., -1]
        s = (s * jnp.exp(g_last)[..., None, None]
             + jnp.einsum('bhck,bhcv->bhkv', kn * jnp.exp(g_last[..., None] - gn)[..., None], v_new))
        return s, o

    xs = tuple(jnp.moveaxis(t, 1, 0) for t in (q, k, u, w, gc, qk))
    s_fin, o = lax.scan(step, s0, xs)
    o = jnp.moveaxis(jnp.moveaxis(o, 0, 1), 2, 3).reshape(B, L, H, DN_DV)
    return o, s_fin


def token_mixer(h, lw, s0, rows, cols):
    B, L, _ = h.shape
    f32 = jnp.float32
    splits = np.cumsum(IN_SIZES)[:-1].tolist()
    qkv, z, bg, ag, u_b, v_b, x_c, gates = jnp.split(h @ lw['w_in'], splits, axis=-1)

    qkv = jax.nn.silu(conv1d_centred(qkv, lw['conv_qkv']))
    q, k, v = jnp.split(qkv, [DN_QK, 2 * DN_QK], axis=-1)
    q = l2norm(q.reshape(B, L, DN_HEADS, DN_DK).astype(f32)) * (DN_DK ** -0.5)
    k = l2norm(k.reshape(B, L, DN_HEADS, DN_DK).astype(f32))
    v = v.reshape(B, L, DN_HEADS, DN_DV).astype(f32)
    beta = jax.nn.sigmoid(bg.astype(f32)).reshape(B, L, 2, DN_HEADS)
    g = -jnp.exp(lw['a_log'].astype(f32)) * jax.nn.softplus(
        ag.astype(f32).reshape(B, L, 2, DN_HEADS) + lw['dt_bias'].astype(f32))
    s0 = s0.astype(f32)
    o_f, s_f = gated_delta_rule(q, k, v, g[:, :, 0], beta[:, :, 0], s0[:, 0])
    flip = lambda t: jnp.flip(t, axis=1)
    o_b, s_b = gated_delta_rule(flip(q), flip(k), flip(v), flip(g[:, :, 1]), flip(beta[:, :, 1]), s0[:, 1])
    o = o_f + flip(o_b)
    o = rmsnorm(o, lw['dn_norm_w']) * jax.nn.silu(z.reshape(B, L, DN_HEADS, DN_DV).astype(f32))
    y_a = o.reshape(B, L, DN_VW).astype(h.dtype) @ lw['p_a']
    new_s = jnp.stack([s_f, s_b], axis=1)

    u_b = jax.nn.gelu(u_b)
    v_b = rmsnorm(jax.nn.gelu(v_b), lw['sgu_norm_w'])
    vg = v_b.reshape(B, L // SGU_CHUNK, SGU_CHUNK, SGU_GROUPS, SGU_WIDTH // SGU_GROUPS)
    sp = jnp.einsum('gpq,bnqgc->bnpgc', lw['w_spatial'], vg) + lw['b_spatial'].T[None, None, :, :, None]
    y_b = (u_b * sp.reshape(B, L, SGU_WIDTH)) @ lw['p_b']

    gw = POOL_WIDTH // POOL_GROUPS
    xcg = x_c.reshape(B, L, POOL_GROUPS, gw)
    pooled = jnp.stack([box_mean_minus_self(xcg[:, :, i], rows, cols, win)
                        for i, win in enumerate(POOL_WINDOWS)], axis=2)
    mixed = jnp.einsum('blgc,gcd->blgd', pooled, lw['pool_w']).reshape(B, L, POOL_WIDTH) * lw['pool_scale']
    y_c = mixed @ lw['p_c']

    gt = jax.nn.sigmoid(gates.astype(f32)).reshape(B, L, N_BRANCH, D_MODEL).astype(h.dtype)
    y = gt[:, :, 0] * y_a + gt[:, :, 1] * y_b + gt[:, :, 2] * y_c
    return y @ lw['w_out'], new_s


def conv_ffn(h, w_up, k_conv, w_down, rows, cols):
    up = depthwise_conv2d(h @ w_up, k_conv, rows, cols)
    gate, val = jnp.split(up, 2, axis=-1)
    return (jax.nn.silu(gate) * val) @ w_down


def trunk_layer(x, cond, lw, s0, rows, cols):
    mod = (jax.nn.silu(cond) @ lw['w_ada'] + lw['b_ada'])[:, None, :]
    sh1, sc1, g1, sh2, sc2, g2 = jnp.split(mod, 6, axis=-1)
    h = rmsnorm(x, lw['norm1_w']) * (1 + sc1) + sh1
    y, s = token_mixer(h, lw, s0, rows, cols)
    x = x + g1 * y
    h = rmsnorm(x, lw['norm2_w']) * (1 + sc2) + sh2
    x = x + g2 * conv_ffn(h, lw['w_up'], lw['conv_ffn'], lw['w_down'], rows, cols)
    return x, s


def setup_inputs(seed: int = 0) -> dict:
    key = jax.random.key(seed)
    ks = jax.random.split(key, 32)
    nrm = lambda k, shape, s: jax.random.normal(k, shape, jnp.float32) * s
    gw = POOL_WIDTH // POOL_GROUPS
    w_in = nrm(ks[5], (DEPTH, D_MODEL, D_IN), D_MODEL ** -0.5)
    lo = 2 * DN_QK + 2 * DN_VW
    w_in = w_in.at[:, :, lo:lo + 4 * DN_HEADS].multiply(0.1)
    dt = jnp.exp(jax.random.uniform(ks[7], (DEPTH, 2, DN_HEADS), jnp.float32, np.log(1e-3), np.log(1e-1)))
    return {
        'x_prompt': nrm(ks[0], (BATCH, SEQ, D_MODEL), 1.0),
        'x_sample': nrm(ks[1], (DEC_BATCH, DEC_SEQ, D_MODEL), 1.0),
        'state_delta': nrm(ks[2], (DEC_BATCH, DEPTH, 2, DN_HEADS, DN_DK, DN_DV), 0.3),
        'c': nrm(ks[3], (DEC_BATCH, D_MODEL), 1.0),
        'c_ctx': nrm(ks[4], (D_MODEL,), 1.0),
        'w_ada': nrm(ks[8], (DEPTH, D_MODEL, 6 * D_MODEL), 0.5 * D_MODEL ** -0.5),
        'b_ada': nrm(ks[9], (DEPTH, 6 * D_MODEL), 0.01),
        'norm1_w': 1.0 + nrm(ks[10], (DEPTH, D_MODEL), 0.02),
        'norm2_w': 1.0 + nrm(ks[11], (DEPTH, D_MODEL), 0.02),
        'w_in': w_in,
        'conv_qkv': nrm(ks[12], (DEPTH, DN_CONV, 2 * DN_QK + DN_VW), DN_CONV ** -0.5),
        'a_log': jnp.log(jax.random.uniform(ks[6], (DEPTH, 2, DN_HEADS), jnp.float32, 1.0, 16.0)),
        'dt_bias': dt + jnp.log(-jnp.expm1(-dt)),
        'dn_norm_w': 1.0 + nrm(ks[13], (DEPTH, DN_DV), 0.02),
        'sgu_norm_w': 1.0 + nrm(ks[14], (DEPTH, SGU_WIDTH), 0.02),
        'w_spatial': nrm(ks[15], (DEPTH, SGU_GROUPS, SGU_CHUNK, SGU_CHUNK), SGU_CHUNK ** -0.5),
        'b_spatial': 1.0 + nrm(ks[16], (DEPTH, SGU_GROUPS, SGU_CHUNK), 0.01),
        'pool_w': nrm(ks[17], (DEPTH, POOL_GROUPS, gw, gw), gw ** -0.5),
        'pool_scale': 1.0 + nrm(ks[18], (DEPTH, POOL_WIDTH), 0.1),
        'p_a': nrm(ks[19], (DEPTH, DN_VW, D_MODEL), DN_VW ** -0.5),
        'p_b': nrm(ks[20], (DEPTH, SGU_WIDTH, D_MODEL), SGU_WIDTH ** -0.5),
        'p_c': nrm(ks[21], (DEPTH, POOL_WIDTH, D_MODEL), POOL_WIDTH ** -0.5),
        'w_out': nrm(ks[22], (DEPTH, D_MODEL, D_MODEL), D_MODEL ** -0.5),
        'w_up': nrm(ks[23], (DEPTH, D_MODEL, 2 * D_FF), D_MODEL ** -0.5),
        'conv_ffn': nrm(ks[24], (DEPTH, FFN_CONV, FFN_CONV, 2 * D_FF), 1.0 / FFN_CONV),
        'w_down': nrm(ks[25], (DEPTH, D_FF, D_MODEL), D_FF ** -0.5),
        'final_norm_w': 1.0 + nrm(ks[26], (D_MODEL,), 0.02),
    }


def reference(x_prompt, x_sample, state_delta, c, c_ctx, w_ada, b_ada, norm1_w, norm2_w, w_in, conv_qkv,
              a_log, dt_bias, dn_norm_w, sgu_norm_w, w_spatial, b_spatial, pool_w, pool_scale, p_a, p_b, p_c,
              w_out, w_up, conv_ffn, w_down, final_norm_w):
    bp, lp, _ = x_prompt.shape
    ls = x_sample.shape[1]
    rows = ls // GRID_W
    xp, xs = x_prompt, x_sample
    zero_state = jnp.zeros((bp, 2, DN_HEADS, DN_DK, DN_DV), jnp.float32)
    ctx_states = []
    for l in range(DEPTH):
        lw = {'w_ada': w_ada[l], 'b_ada': b_ada[l], 'norm1_w': norm1_w[l], 'norm2_w': norm2_w[l],
              'w_in': w_in[l], 'conv_qkv': conv_qkv[l], 'a_log': a_log[l], 'dt_bias': dt_bias[l],
              'dn_norm_w': dn_norm_w[l], 'sgu_norm_w': sgu_norm_w[l], 'w_spatial': w_spatial[l],
              'b_spatial': b_spatial[l], 'pool_w': pool_w[l], 'pool_scale': pool_scale[l],
              'p_a': p_a[l], 'p_b': p_b[l], 'p_c': p_c[l], 'w_out': w_out[l],
              'w_up': w_up[l], 'conv_ffn': conv_ffn[l], 'w_down': w_down[l]}
        xp, s_ctx = trunk_layer(xp, c_ctx[None, :], lw, zero_state, 1, lp)
        ctx_states.append(s_ctx)
        xs, _ = trunk_layer(xs, c, lw, state_delta[:, l], rows, GRID_W)
    new_state_delta = jnp.stack(ctx_states, axis=1).astype(x_prompt.dtype)
    y_prompt = rmsnorm(xp, final_norm_w)
    y_sample = rmsnorm(xs, final_norm_w)
    return (y_prompt, y_sample, new_state_delta)
```

```python
import functools

import numpy as np
import jax
import jax.numpy as jnp
from jax import lax
from jax.experimental import pallas as pl
from jax.experimental.pallas import tpu as pltpu

F32 = jnp.float32
BF16 = jnp.bfloat16

D_MODEL = 2048
DEPTH = 2
GRID_W = 64
DN_HEADS = 8
DN_DK = 128
DN_DV = 128
DN_QK = DN_HEADS * DN_DK
DN_VW = DN_HEADS * DN_DV
DN_CONV = 5
DN_CHUNK = 64
SGU_WIDTH = 1024
SGU_GROUPS = 4
SGU_CHUNK = 128
POOL_WIDTH = 1024
POOL_GROUPS = 4
D_FF = 5632
N_BRANCH = 3
EPS = 1e-6

SEG = 256
N_GATE_COLS = 4 * DN_HEADS
C_QKV = 0
C_Z = 2 * DN_QK + DN_VW
C_U = C_Z + DN_VW
C_V = C_U + SGU_WIDTH
C_XC = C_V + SGU_WIDTH
C_G = C_XC + POOL_WIDTH
N_MAIN = C_G + N_BRANCH * D_MODEL

VMEM_LIMIT = 52 << 20


def _params(n_axes, vmem=VMEM_LIMIT):
    return pltpu.CompilerParams(dimension_semantics=("arbitrary",) * n_axes, vmem_limit_bytes=vmem)


def _sigmoid(x):
    return 1.0 / (1.0 + jnp.exp(-x))


def _silu(x):
    return x * _sigmoid(x)


def _gelu_tanh(x):
    return x * (0.5 * (1.0 + jnp.tanh(np.sqrt(2.0 / np.pi).astype(np.float32) * (x + 0.044715 * (x * x * x)))))


def _dot(a, b):
    return jnp.dot(a, b, preferred_element_type=F32)


def _mod_kernel(c_ref, w_ref, b_ref, o_ref):
    a = _silu(c_ref[...]).astype(BF16)
    o_ref[0] = _dot(a, w_ref[0].astype(BF16)) + b_ref[0]


def _modulation(cond, w_ada, b_ada, tn=1024):
    depth, d, n = w_ada.shape
    m = cond.shape[0]
    return pl.pallas_call(
        _mod_kernel,
        out_shape=jax.ShapeDtypeStruct((depth, m, n), F32),
        grid=(depth, n // tn),
        in_specs=[pl.BlockSpec((m, d), lambda l, j: (0, 0)),
                  pl.BlockSpec((1, d, tn), lambda l, j: (l, 0, j)),
                  pl.BlockSpec((1, 1, tn), lambda l, j: (l, 0, j))],
        out_specs=pl.BlockSpec((1, m, tn), lambda l, j: (l, 0, j)),
        compiler_params=_params(2),
        name="adaln_mod",
    )(cond, w_ada, b_ada.reshape(depth, 1, n))


def _norm_mm_kernel(*refs, nseg, has_small):
    if has_small:
        x_ref, nw_ref, sc_ref, sh_ref, w_ref, ws_ref, o_ref, os_ref, h_ref = refs
    else:
        x_ref, nw_ref, sc_ref, sh_ref, w_ref, o_ref, h_ref = refs

    @pl.when(pl.program_id(1) == 0)
    def _():
        for s in range(nseg):
            xs = x_ref[s * SEG:(s + 1) * SEG, :]
            y = xs * lax.rsqrt(jnp.mean(xs * xs, axis=-1, keepdims=True) + EPS) * nw_ref[...]
            h_ref[s * SEG:(s + 1) * SEG, :] = (y * (1.0 + sc_ref[s]) + sh_ref[s]).astype(BF16)
        if has_small:
            os_ref[...] = _dot(h_ref[...], ws_ref[...])

    o_ref[...] = _dot(h_ref[...], w_ref[...]).astype(o_ref.dtype)


def _norm_matmul(x, norm_w, scale, shift, w, w_small=None, tm=1024, tn=1024):
    t, d = x.shape
    n = w.shape[1]
    nseg = tm // SEG
    has_small = w_small is not None
    in_specs = [pl.BlockSpec((tm, d), lambda i, j: (i, 0)),
                pl.BlockSpec((1, d), lambda i, j: (0, 0)),
                pl.BlockSpec((nseg, 1, d), lambda i, j: (i, 0, 0)),
                pl.BlockSpec((nseg, 1, d), lambda i, j: (i, 0, 0)),
                pl.BlockSpec((d, tn), lambda i, j: (0, j))]
    args = [x, norm_w.reshape(1, d), scale, shift, w]
    out_shape = [jax.ShapeDtypeStruct((t, n), BF16)]
    out_specs = [pl.BlockSpec((tm, tn), lambda i, j: (i, j))]
    if has_small:
        ns = w_small.shape[1]
        in_specs.append(pl.BlockSpec((d, ns), lambda i, j: (0, 0)))
        args.append(w_small)
        out_shape.append(jax.ShapeDtypeStruct((t, ns), F32))
        out_specs.append(pl.BlockSpec((tm, ns), lambda i, j: (i, 0)))
    out = pl.pallas_call(
        functools.partial(_norm_mm_kernel, nseg=nseg, has_small=has_small),
        out_shape=out_shape,
        grid=(t // tm, n // tn),
        in_specs=in_specs,
        out_specs=out_specs,
        scratch_shapes=[pltpu.VMEM((tm, d), BF16)],
        compiler_params=_params(2),
        name="norm_matmul_small" if has_small else "norm_matmul",
    )(*args)
    return out if has_small else out[0]


def _split3(x):
    hi = x.astype(BF16).astype(F32)
    r1 = x - hi
    mid = r1.astype(BF16).astype(F32)
    lo = (r1 - mid).astype(BF16).astype(F32)
    return hi, mid, lo


def _gates_kernel(s_ref, alog_ref, dtb_ref, o_ref, *, tb):
    H = DN_HEADS
    nh2 = 2 * H
    log_c = int(np.log2(DN_CHUNK))
    bg = s_ref[0:nh2, :]
    ag = s_ref[nh2:2 * nh2, :] + dtb_ref[...]
    softplus = jnp.maximum(ag, 0.0) + jnp.log1p(jnp.exp(-jnp.abs(ag)))
    g = -jnp.exp(alog_ref[...]) * softplus
    o_ref[0:nh2, :] = _sigmoid(bg)
    ii = lax.broadcasted_iota(jnp.int32, (128, 128), 0)
    jj = lax.broadcasted_iota(jnp.int32, (128, 128), 1)
    same = lax.shift_right_logical(ii, log_c) == lax.shift_right_logical(jj, log_c)
    m_fwd = jnp.where(same, jnp.where(ii <= jj, 1.0, 0.0), 0.0).astype(BF16)
    m_bwd = jnp.where(same, jnp.where(ii >= jj, 1.0, 0.0), 0.0).astype(BF16)
    m_all = jnp.where(same, 1.0, 0.0).astype(BF16)
    parts = _split3(g)
    for c in range(tb // 128):
        lanes = slice(c * 128, (c + 1) * 128)
        pf = jnp.concatenate([p[0:H, lanes] for p in parts], axis=0).astype(BF16)
        pb = jnp.concatenate([p[H:nh2, lanes] for p in parts], axis=0).astype(BF16)
        pa = jnp.concatenate([p[:, lanes] for p in parts], axis=0).astype(BF16)
        cf = _dot(pf, m_fwd)
        cb = _dot(pb, m_bwd)
        ca = _dot(pa, m_all)
        o_ref[nh2:nh2 + H, lanes] = cf[0:H] + cf[H:2 * H] + cf[2 * H:3 * H]
        o_ref[nh2 + H:2 * nh2, lanes] = cb[0:H] + cb[H:2 * H] + cb[2 * H:3 * H]
        o_ref[2 * nh2:3 * nh2, lanes] = ca[0:nh2] + ca[nh2:2 * nh2] + ca[2 * nh2:3 * nh2]


def _gates(small_t, a_log, dt_bias, tb=2048):
    nh2 = 2 * DN_HEADS
    t = small_t.shape[1]
    return pl.pallas_call(
        functools.partial(_gates_kernel, tb=tb),
        out_shape=jax.ShapeDtypeStruct((3 * nh2, t), F32),
        grid=(t // tb,),
        in_specs=[pl.BlockSpec((2 * nh2, tb), lambda i: (0, i)),
                  pl.BlockSpec((nh2, 1), lambda i: (0, 0)),
                  pl.BlockSpec((nh2, 1), lambda i: (0, 0))],
        out_specs=pl.BlockSpec((3 * nh2, tb), lambda i: (0, i)),
        compiler_params=_params(1),
        name="dn_gates",
    )(small_t, a_log.reshape(nh2, 1), dt_bias.reshape(nh2, 1))


def _dn_kernel(*refs, L, has_s0, emit_state):
    C = DN_CHUNK
    n_chunks = L // C
    refs = list(refs)
    q_ref, k_ref, v_ref, z_ref, cq_ref, ck_ref, cv_ref, col_ref, row_ref, nw_ref = refs[:10]
    pos = 10
    s0_ref = None
    if has_s0:
        s0_ref = refs[pos]
        pos += 1
    o_ref = refs[pos]
    pos += 1
    st_ref = None
    if emit_state:
        st_ref = refs[pos]
        pos += 1
    xp, qs, ks, vs, oacc, s_scr = refs[pos:pos + 6]

    half = DN_CONV // 2
    pad = 8

    def prep(raw_ref, cw_ref, dst_ref, kind):
        xp[0:pad, :] = jnp.zeros((pad, DN_DK), F32)
        xp[pad + L:2 * pad + L, :] = jnp.zeros((pad, DN_DK), F32)
        xp[pad:pad + L, :] = raw_ref[...].astype(F32)
        rb = min(L, 256)
        for b in range(L // rb):
            acc = None
            for j in range(DN_CONV):
                st = b * rb + pad - half + j
                term = xp[st:st + rb, :] * cw_ref[j:j + 1, :]
                acc = term if acc is None else acc + term
            y = _silu(acc)
            if kind != "v":
                y = y * lax.rsqrt(jnp.sum(y * y, axis=-1, keepdims=True) + EPS)
            if kind == "q":
                y = y * (DN_DK ** -0.5)
            dst_ref[b * rb:(b + 1) * rb, :] = y

    prep(q_ref, cq_ref, qs, "q")
    prep(k_ref, ck_ref, ks, "k")
    prep(v_ref, cv_ref, vs, "v")
    oacc[...] = jnp.zeros((L, DN_DV), F32)
    for d in range(2):
        s_scr[d] = s0_ref[0, d, 0] if has_s0 else jnp.zeros((DN_DK, DN_DV), F32)

    ii = lax.broadcasted_iota(jnp.int32, (C, C), 0)
    jj = lax.broadcasted_iota(jnp.int32, (C, C), 1)

    def chunk(d, cidx):
        r0 = pl.multiple_of(cidx * C, C)
        qc = qs[pl.ds(r0, C), :]
        kc = ks[pl.ds(r0, C), :]
        vc = vs[pl.ds(r0, C), :]
        colv = col_ref[0, cidx]
        rowv = row_ref[0, cidx]
        beta = colv[:, 3 * d:3 * d + 1]
        gcc = colv[:, 3 * d + 1:3 * d + 2]
        glc = colv[:, 3 * d + 2:3 * d + 3]
        gcr = rowv[3 * d + 1:3 * d + 2, :]
        incl = (jj <= ii) if d == 0 else (jj >= ii)
        strict = (jj < ii) if d == 0 else (jj > ii)
        decay = jnp.exp(jnp.where(incl, gcc - gcr, -jnp.inf))
        kb = kc * beta
        egc = jnp.exp(gcc)
        kq = jnp.concatenate([kb, qc], axis=0).astype(BF16)
        gram = lax.dot_general(kq, kc.astype(BF16), (((1,), (1,)), ((), ())), preferred_element_type=F32)
        a = jnp.where(strict, gram[:C] * decay, 0.0)
        qk = gram[C:] * decay
        a16 = a.astype(BF16)
        m = _dot(a16, a16)
        nm = -a
        n_sq = int(np.log2(C)) - 1
        for r in range(n_sq):
            m16 = m.astype(BF16)
            if r < n_sq - 1:
                p = _dot(jnp.concatenate([nm, m], axis=0).astype(BF16), m16)
                nm = nm + m + p[:C]
                m = p[C:]
            else:
                nm = nm + m + _dot(nm.astype(BF16), m16)
        rhs = jnp.concatenate([vc * beta, kb * egc], axis=1)
        sol = rhs + _dot(nm.astype(BF16), rhs.astype(BF16))
        u = sol[:, :DN_DV]
        w = sol[:, DN_DV:]
        s = s_scr[d]
        wq = jnp.concatenate([w, qc * egc], axis=0).astype(BF16)
        ws = _dot(wq, s.astype(BF16))
        v_new = u - ws[:C]
        v16 = v_new.astype(BF16)
        o = ws[C:] + _dot(qk.astype(BF16), v16)
        kd = (kc * jnp.exp(glc - gcc)).astype(BF16)
        s_scr[d] = s * jnp.exp(glc[0:1, :]) + lax.dot_general(
            kd, v16, (((0,), (0,)), ((), ())), preferred_element_type=F32)
        oacc[pl.ds(r0, C), :] += o

    def body(n, carry):
        chunk(0, n)
        chunk(1, n_chunks - 1 - n)
        return carry

    lax.fori_loop(0, n_chunks, body, 0)

    if emit_state:
        for d in range(2):
            st_ref[0, d, 0] = s_scr[d]
    o = oacc[...]
    y = o * lax.rsqrt(jnp.mean(o * o, axis=-1, keepdims=True) + EPS) * nw_ref[...]
    o_ref[...] = (y * _silu(z_ref[...].astype(F32))).astype(BF16)


def _deltanet(main, conv_w, gcol, grow, dn_norm_w, s0, *, L, n_seq, rb0):
    H = DN_HEADS
    has_s0 = s0 is not None
    emit_state = not has_s0
    nc = L // DN_CHUNK
    qb, kb_, vb, zb = C_QKV // DN_DK, (C_QKV + DN_QK) // DN_DK, (C_QKV + 2 * DN_QK) // DN_DK, C_Z // DN_DV
    in_specs = [pl.BlockSpec((L, DN_DK), lambda s, h: (rb0 + s, qb + h)),
                pl.BlockSpec((L, DN_DK), lambda s, h: (rb0 + s, kb_ + h)),
                pl.BlockSpec((L, DN_DV), lambda s, h: (rb0 + s, vb + h)),
                pl.BlockSpec((L, DN_DV), lambda s, h: (rb0 + s, zb + h)),
                pl.BlockSpec((8, DN_DK), lambda s, h: (0, h)),
                pl.BlockSpec((8, DN_DK), lambda s, h: (0, H + h)),
                pl.BlockSpec((8, DN_DV), lambda s, h: (0, 2 * H + h)),
                pl.BlockSpec((1, nc, DN_CHUNK, 8), lambda s, h: (h, rb0 + s, 0, 0)),
                pl.BlockSpec((1, nc, 8, DN_CHUNK), lambda s, h: (h, rb0 + s, 0, 0)),
                pl.BlockSpec((1, DN_DV), lambda s, h: (0, 0))]
    args = [main, main, main, main, conv_w, conv_w, conv_w, gcol, grow, dn_norm_w.reshape(1, DN_DV)]
    if has_s0:
        in_specs.append(pl.BlockSpec((1, 2, 1, DN_DK, DN_DV), lambda s, h: (s, 0, h, 0, 0)))
        args.append(s0)
    out_shape = [jax.ShapeDtypeStruct((n_seq * L, DN_VW), BF16)]
    out_specs = [pl.BlockSpec((L, DN_DV), lambda s, h: (s, h))]
    if emit_state:
        out_shape.append(jax.ShapeDtypeStruct((n_seq, 2, H, DN_DK, DN_DV), F32))
        out_specs.append(pl.BlockSpec((1, 2, 1, DN_DK, DN_DV), lambda s, h: (s, 0, h, 0, 0)))
    out = pl.pallas_call(
        functools.partial(_dn_kernel, L=L, has_s0=has_s0, emit_state=emit_state),
        out_shape=out_shape,
        grid=(n_seq, H),
        in_specs=in_specs,
        out_specs=out_specs,
        scratch_shapes=[pltpu.VMEM((L + 16, DN_DK), F32), pltpu.VMEM((L, DN_DK), F32),
                        pltpu.VMEM((L, DN_DK), F32), pltpu.VMEM((L, DN_DV), F32),
                        pltpu.VMEM((L, DN_DV), F32), pltpu.VMEM((2, DN_DK, DN_DV), F32)],
        compiler_params=_params(2),
        name="deltanet_ctx" if emit_state else "deltanet_latent",
    )(*args)
    return (out[0], out[1]) if emit_state else (out[0], None)


def _sgu_kernel(u_ref, v_ref, nw_ref, ws_ref, bt_ref, o_ref, *, tb):
    gw = SGU_WIDTH // SGU_GROUPS
    v = _gelu_tanh(v_ref[...].astype(F32))
    v = (v * lax.rsqrt(jnp.mean(v * v, axis=-1, keepdims=True) + EPS) * nw_ref[...]).astype(BF16)
    for c in range(tb // SGU_CHUNK):
        rows = slice(c * SGU_CHUNK, (c + 1) * SGU_CHUNK)
        for g in range(SGU_GROUPS):
            cols = slice(g * gw, (g + 1) * gw)
            sp = _dot(ws_ref[g], v[rows, cols]) + bt_ref[:, g:g + 1]
            u = _gelu_tanh(u_ref[rows, cols].astype(F32))
            o_ref[rows, cols] = (u * sp).astype(BF16)


def _sgu(main, sgu_norm_w, w_spatial, b_spatial_t, tb=512):
    t = main.shape[0]
    ub, vb = C_U // SGU_WIDTH, C_V // SGU_WIDTH
    return pl.pallas_call(
        functools.partial(_sgu_kernel, tb=tb),
        out_shape=jax.ShapeDtypeStruct((t, SGU_WIDTH), BF16),
        grid=(t // tb,),
        in_specs=[pl.BlockSpec((tb, SGU_WIDTH), lambda i: (i, ub)),
                  pl.BlockSpec((tb, SGU_WIDTH), lambda i: (i, vb)),
                  pl.BlockSpec((1, SGU_WIDTH), lambda i: (0, 0)),
                  pl.BlockSpec((SGU_GROUPS, SGU_CHUNK, SGU_CHUNK), lambda i: (0, 0, 0)),
                  pl.BlockSpec((SGU_CHUNK, SGU_GROUPS), lambda i: (0, 0))],
        out_specs=pl.BlockSpec((tb, SGU_WIDTH), lambda i: (i, 0)),
        compiler_params=_params(1),
        name="sgu",
    )(main, main, sgu_norm_w.reshape(1, SGU_WIDTH), w_spatial, b_spatial_t)


def _pool_kernel(x_ref, pw_ref, sc_ref, o_ref, b_scr, ic_scr, *, L, rows, cols):
    g = pl.program_id(0)
    log_cols = int(np.log2(cols))
    strip = 128

    @pl.when(pl.program_id(1) == 0)
    def _():
        half = lax.shift_left(jnp.int32(1), g)

        def build(sidx, carry):
            r0 = pl.multiple_of(sidx * strip, strip)
            ti = lax.broadcasted_iota(jnp.int32, (strip, L), 0) + r0
            tj = lax.broadcasted_iota(jnp.int32, (strip, L), 1)
            dr = lax.shift_right_logical(tj, log_cols) - lax.shift_right_logical(ti, log_cols) + half
            dc = (tj & (cols - 1)) - (ti & (cols - 1)) + half
            in_r = (dr >= 0) & (dr < 2 * half)
            in_c = (dc >= 0) & (dc < 2 * half)
            b_scr[pl.ds(r0, strip), :] = jnp.where(in_r, jnp.where(in_c, 1.0, 0.0), 0.0).astype(BF16)
            return carry

        lax.fori_loop(0, L // strip, build, 0)
        t = lax.broadcasted_iota(jnp.int32, (L, 1), 0)
        r = lax.shift_right_logical(t, log_cols)
        c = t & (cols - 1)
        cnt = ((jnp.minimum(r + half, rows) - jnp.maximum(r - half, 0))
               * (jnp.minimum(c + half, cols) - jnp.maximum(c - half, 0)))
        ic_scr[...] = cnt.astype(F32)

    x = x_ref[...]
    pooled = _dot(b_scr[...], x) / ic_scr[...] - x.astype(F32)
    o_ref[...] = (_dot(pooled.astype(BF16), pw_ref[0]) * sc_ref[...]).astype(BF16)


def _pool(main, pool_w, pool_scale, *, L, n_seq, rb0, rows, cols):
    gw = POOL_WIDTH // POOL_GROUPS
    xb = C_XC // gw
    return pl.pallas_call(
        functools.partial(_pool_kernel, L=L, rows=rows, cols=cols),
        out_shape=jax.ShapeDtypeStruct((n_seq * L, POOL_WIDTH), BF16),
        grid=(POOL_GROUPS, n_seq),
        in_specs=[pl.BlockSpec((L, gw), lambda g, s: (rb0 + s, xb + g)),
                  pl.BlockSpec((1, gw, gw), lambda g, s: (g, 0, 0)),
                  pl.BlockSpec((1, gw), lambda g, s: (0, g))],
        out_specs=pl.BlockSpec((L, gw), lambda g, s: (s, g)),
        scratch_shapes=[pltpu.VMEM((L, L), BF16), pltpu.VMEM((L, 1), F32)],
        compiler_params=_params(2),
        name="pool_ctx" if rows == 1 else "pool_latent",
    )(main, pool_w, pool_scale.reshape(1, POOL_WIDTH))


def _merge_kernel(a_ref, b_ref, c_ref, pa_ref, pb_ref, pc_ref, ga_ref, gb_ref, gc_ref, o_ref):
    y = _sigmoid(ga_ref[...].astype(F32)) * _dot(a_ref[...], pa_ref[...])
    y = y + _sigmoid(gb_ref[...].astype(F32)) * _dot(b_ref[...], pb_ref[...])
    y = y + _sigmoid(gc_ref[...].astype(F32)) * _dot(c_ref[...], pc_ref[...])
    o_ref[...] = y.astype(BF16)


def _merge(ya, yb, yc, p_a, p_b, p_c, main, tm=1024, tn=512):
    t = ya.shape[0]
    gb = C_G // tn
    gstep = D_MODEL // tn
    act = lambda k: pl.BlockSpec((tm, k), lambda i, j: (i, 0))
    wsp = lambda k: pl.BlockSpec((k, tn), lambda i, j: (0, j))
    gsp = lambda b: pl.BlockSpec((tm, tn), lambda i, j: (i, gb + b * gstep + j))
    return pl.pallas_call(
        _merge_kernel,
        out_shape=jax.ShapeDtypeStruct((t, D_MODEL), BF16),
        grid=(t // tm, D_MODEL // tn),
        in_specs=[act(DN_VW), act(SGU_WIDTH), act(POOL_WIDTH), wsp(DN_VW), wsp(SGU_WIDTH), wsp(POOL_WIDTH),
                  gsp(0), gsp(1), gsp(2)],
        out_specs=pl.BlockSpec((tm, tn), lambda i, j: (i, j)),
        compiler_params=_params(2),
        name="merge",
    )(ya, yb, yc, p_a, p_b, p_c, main, main, main)


def _mm_res_kernel(a_ref, w_ref, x_ref, g_ref, o_ref, acc_ref, *, nseg, nk):
    k = pl.program_id(2)
    part = _dot(a_ref[...], w_ref[...])

    @pl.when(k == 0)
    def _():
        acc_ref[...] = part

    @pl.when(k > 0)
    def _():
        acc_ref[...] += part

    @pl.when(k == nk - 1)
    def _():
        for s in range(nseg):
            rows = slice(s * SEG, (s + 1) * SEG)
            o_ref[rows, :] = x_ref[rows, :] + g_ref[s] * acc_ref[rows, :]


def _matmul_residual(a, w, x, gate, tm=1024, tn=1024, tk=None):
    t, kdim = a.shape
    n = w.shape[1]
    tk = kdim if tk is None else tk
    nk = kdim // tk
    nseg = tm // SEG
    return pl.pallas_call(
        functools.partial(_mm_res_kernel, nseg=nseg, nk=nk),
        out_shape=jax.ShapeDtypeStruct((t, n), F32),
        grid=(t // tm, n // tn, nk),
        in_specs=[pl.BlockSpec((tm, tk), lambda i, j, k: (i, k)),
                  pl.BlockSpec((tk, tn), lambda i, j, k: (k, j)),
                  pl.BlockSpec((tm, tn), lambda i, j, k: (i, j)),
                  pl.BlockSpec((nseg, 1, tn), lambda i, j, k: (i, 0, j))],
        out_specs=pl.BlockSpec((tm, tn), lambda i, j, k: (i, j)),
        scratch_shapes=[pltpu.VMEM((tm, tn), F32)],
        compiler_params=_params(3),
        name="matmul_residual",
    )(a, w, x, gate)


def _conv_glu_kernel(g_ref, v_ref, kg_ref, kv_ref, o_ref, xg, xv, *, tb, n_ctx_blocks, ctx_cols, rows, cols):
    pad = 2 * cols if rows > 1 else 8
    pad = max(pad, 8)
    rb = 64

    def run(n_rows, n_cols):
        off = n_cols if n_rows > 1 else 0
        taps_r = (0, 1, 2) if n_rows > 1 else (1,)
        t = lax.broadcasted_iota(jnp.int32, (tb, 1), 0)
        c = t & (n_cols - 1)
        mask_l = jnp.where(c != 0, 1.0, 0.0).astype(F32)
        mask_r = jnp.where(c != n_cols - 1, 1.0, 0.0).astype(F32)

        def stage(src_ref, scr):
            for s in range(3):
                scr[s, 0:pad, :] = jnp.zeros((pad, scr.shape[2]), F32)
                scr[s, pad + tb:2 * pad + tb, :] = jnp.zeros((pad, scr.shape[2]), F32)
            scr[1, pad:pad + tb, :] = src_ref[...].astype(F32)
            scr[0, pad:pad + tb, :] = scr[1, pad - 1:pad - 1 + tb, :] * mask_l
            scr[2, pad:pad + tb, :] = scr[1, pad + 1:pad + 1 + tb, :] * mask_r

        stage(g_ref, xg)
        stage(v_ref, xv)

        def conv(scr, k_ref, r0):
            acc = None
            for dr in taps_r:
                for dc in range(3):
                    term = scr[dc, pl.ds(r0 + pad + (dr - 1) * off, rb), :] * k_ref[3 * dr + dc:3 * dr + dc + 1, :]
                    acc = term if acc is None else acc + term
            return acc

        def body(i, carry):
            r0 = pl.multiple_of(i * rb, rb)
            o_ref[pl.ds(r0, rb), :] = (_silu(conv(xg, kg_ref, r0)) * conv(xv, kv_ref, r0)).astype(BF16)
            return carry

        lax.fori_loop(0, tb // rb, body, 0)

    is_ctx = pl.program_id(0) < n_ctx_blocks

    @pl.when(is_ctx)
    def _():
        run(1, ctx_cols)

    @pl.when(jnp.logical_not(is_ctx))
    def _():
        run(rows, cols)


def _conv_glu(up, k9, *, tb, n_ctx_blocks, ctx_cols, rows, cols, ct=256):
    t = up.shape[0]
    nj = D_FF // ct
    pad = max(2 * cols if rows > 1 else 8, 8)
    return pl.pallas_call(
        functools.partial(_conv_glu_kernel, tb=tb, n_ctx_blocks=n_ctx_blocks, ctx_cols=ctx_cols,
                          rows=rows, cols=cols),
        out_shape=jax.ShapeDtypeStruct((t, D_FF), BF16),
        grid=(t // tb, nj),
        in_specs=[pl.BlockSpec((tb, ct), lambda i, j: (i, j)),
                  pl.BlockSpec((tb, ct), lambda i, j: (i, nj + j)),
                  pl.BlockSpec((16, ct), lambda i, j: (0, j)),
                  pl.BlockSpec((16, ct), lambda i, j: (0, nj + j))],
        out_specs=pl.BlockSpec((tb, ct), lambda i, j: (i, j)),
        scratch_shapes=[pltpu.VMEM((3, tb + 2 * pad, ct), F32), pltpu.VMEM((3, tb + 2 * pad, ct), F32)],
        compiler_params=_params(2),
        name="conv_glu",
    )(up, up, k9, k9)


def _final_norm_kernel(x_ref, w_ref, o_ref):
    x = x_ref[...]
    o_ref[...] = x * lax.rsqrt(jnp.mean(x * x, axis=-1, keepdims=True) + EPS) * w_ref[...]


def _final_norm(x, w, *, row0, n_rows, tb=512):
    d = x.shape[1]
    b0 = row0 // tb
    return pl.pallas_call(
        _final_norm_kernel,
        out_shape=jax.ShapeDtypeStruct((n_rows, d), F32),
        grid=(n_rows // tb,),
        in_specs=[pl.BlockSpec((tb, d), lambda i: (b0 + i, 0)),
                  pl.BlockSpec((1, d), lambda i: (0, 0))],
        out_specs=pl.BlockSpec((tb, d), lambda i: (i, 0)),
        compiler_params=_params(1),
        name="final_norm",
    )(x, w.reshape(1, d))


def _gate_layouts(gates_t):
    H = DN_HEADS
    t = gates_t.shape[1]
    g = gates_t.reshape(3, 2, H, t).transpose(2, 1, 0, 3).reshape(H, 6, t)
    g = jnp.pad(g, ((0, 0), (0, 2), (0, 0)))
    grow = g.reshape(H, 8, t // DN_CHUNK, DN_CHUNK).transpose(0, 2, 1, 3)
    gcol = grow.transpose(0, 1, 3, 2)
    return gcol, grow


def kernel(x_prompt, x_sample, state_delta, c, c_ctx, w_ada, b_ada, norm1_w, norm2_w, w_in, conv_qkv, a_log, dt_bias, dn_norm_w, sgu_norm_w, w_spatial, b_spatial, pool_w, pool_scale, p_a, p_b, p_c, w_out, w_up, conv_ffn, w_down, final_norm_w):
    bp, lp, d = x_prompt.shape
    bs, ls, _ = x_sample.shape
    tp, ts = bp * lp, bs * ls
    t = tp + ts
    rows = ls // GRID_W
    assert lp % SEG == 0 and ls % SEG == 0 and ls % lp == 0 and tp % ls == 0

    x = jnp.concatenate([x_prompt.reshape(tp, d), x_sample.reshape(ts, d)], axis=0)

    n_cond = -(-(bs + 1) // 8) * 8
    cond = jnp.zeros((n_cond, d), F32).at[:bs].set(c).at[bs].set(c_ctx)
    mod = _modulation(cond, w_ada, b_ada)
    seg_rows = np.concatenate([np.full(tp // SEG, bs), np.repeat(np.arange(bs), ls // SEG)]).astype(np.int32)

    ctx_states = []
    for l in range(DEPTH):
        m = mod[l][seg_rows].reshape(t // SEG, 6, 1, d)
        sh1, sc1, g1, sh2, sc2, g2 = (m[:, k] for k in range(6))

        lo = C_Z + DN_VW
        w_main = jnp.concatenate([w_in[l][:, :lo], w_in[l][:, lo + N_GATE_COLS:]], axis=1).astype(BF16)
        w_small = jnp.pad(w_in[l][:, lo:lo + N_GATE_COLS], ((0, 0), (0, 128 - N_GATE_COLS))).astype(BF16)
        main, small = _norm_matmul(x, norm1_w[l], sc1, sh1, w_main, w_small)

        gates_t = _gates(small[:, :N_GATE_COLS].T, a_log[l], dt_bias[l])
        gcol, grow = _gate_layouts(gates_t)
        conv_w = jnp.pad(conv_qkv[l], ((0, 8 - DN_CONV), (0, 0)))
        ya_p, s_ctx = _deltanet(main, conv_w, gcol, grow, dn_norm_w[l], None, L=lp, n_seq=bp, rb0=0)
        ya_s, _ = _deltanet(main, conv_w, gcol, grow, dn_norm_w[l], state_delta[:, l], L=ls, n_seq=bs,
                            rb0=tp // ls)
        ctx_states.append(s_ctx)
        ya = jnp.concatenate([ya_p, ya_s], axis=0)

        yb = _sgu(main, sgu_norm_w[l], w_spatial[l].astype(BF16), b_spatial[l].T)

        pw = pool_w[l].astype(BF16)
        yc = jnp.concatenate([
            _pool(main, pw, pool_scale[l], L=lp, n_seq=bp, rb0=0, rows=1, cols=lp),
            _pool(main, pw, pool_scale[l], L=ls, n_seq=bs, rb0=tp // ls, rows=rows, cols=GRID_W)], axis=0)

        y = _merge(ya, yb, yc, p_a[l].astype(BF16), p_b[l].astype(BF16), p_c[l].astype(BF16), main)
        x = _matmul_residual(y, w_out[l].astype(BF16), x, g1)

        up = _norm_matmul(x, norm2_w[l], sc2, sh2, w_up[l].astype(BF16))
        k9 = jnp.pad(conv_ffn[l].reshape(9, 2 * D_FF), ((0, 7), (0, 0)))
        act = _conv_glu(up, k9, tb=ls, n_ctx_blocks=tp // ls, ctx_cols=lp, rows=rows, cols=GRID_W)
        x = _matmul_residual(act, w_down[l].astype(BF16), x, g2, tk=D_FF // 4)

    y_prompt = _final_norm(x, final_norm_w, row0=0, n_rows=tp).reshape(bp, lp, d)
    y_sample = _final_norm(x, final_norm_w, row0=tp, n_rows=ts).reshape(bs, ls, d)
    new_state = jnp.stack(ctx_states, axis=1).astype(x_prompt.dtype)
    return (y_prompt, y_sample, new_state)
```
